```python
import jax, jax.numpy as jnp
from jax import lax
import numpy as np

D_MODEL = 1024
BATCH = 16
SEQ = 256
DEPTH = 2
DEC_BATCH = 2
DEC_SEQ = 1024
PAST_LEN = 256

GRID_W = 64
GROUP_W = D_MODEL // 4
HEAD_DIM = 64
MIX_W = 4 * GROUP_W
CONV_W = GROUP_W
MLA_HEADS = GROUP_W // HEAD_DIM
MLA_NOPE = HEAD_DIM
MLA_ROPE = HEAD_DIM // 2
MLA_V = HEAD_DIM
Q_LORA = D_MODEL // 4
KV_LORA = D_MODEL // 8
RET_HEADS = GROUP_W // HEAD_DIM
RET_DK = HEAD_DIM
RET_DV = HEAD_DIM
RET_CHUNK = 64
POOL_WINDOWS = (2, 4, 8, 16)
POOL_GROUPS = 4
POOL_CH = GROUP_W // POOL_GROUPS
PEER_HEADS = 8
PEER_N_KEYS = 128
PEER_EXPERTS = PEER_N_KEYS * PEER_N_KEYS
PEER_TOPK = 16
PEER_QDIM = 128
PEER_HALF = PEER_QDIM // 2
ROPE_BASE = 10000.0
AXIS_PAIRS = MLA_ROPE // 4
Q_BLOCK = 128
TOKEN_BLOCK = 128
EPS = 1e-6
OFF_MLA = 3 * CONV_W
OFF_RET = OFF_MLA + Q_LORA + KV_LORA + MLA_ROPE
OFF_POOL = OFF_RET + 4 * GROUP_W
IN_COLS = OFF_POOL + GROUP_W

kernel_name = 'hybrid_diffusion_prefix_step'


def rmsnorm(x, g):
    xf = x.astype(jnp.float32)
    y = xf * lax.rsqrt(jnp.mean(xf * xf, axis=-1, keepdims=True) + EPS)
    return (y * g.astype(jnp.float32)).astype(x.dtype)


def modulation(cvec, w, b):
    m = (jax.nn.silu(cvec) @ w + b)[..., None, :]
    return jnp.split(m, 6, axis=-1)


def rope_tables(L):
    rows = L // GRID_W
    row = jnp.repeat(jnp.arange(rows, dtype=jnp.float32), GRID_W)
    col = jnp.tile(jnp.arange(GRID_W, dtype=jnp.float32), rows)
    inv = ROPE_BASE ** (-jnp.arange(AXIS_PAIRS, dtype=jnp.float32) / AXIS_PAIRS)
    ang_r = row[:, None] * inv
    ang_c = col[:, None] * inv
    ang = jnp.concatenate([ang_r, ang_r, ang_c, ang_c], axis=-1)
    return jnp.cos(ang), jnp.sin(ang)


def axial_rotate(x, cos, sin):
    xa = x.reshape(x.shape[:-1] + (2, 2, AXIS_PAIRS))
    rot = jnp.stack([-xa[..., 1, :], xa[..., 0, :]], axis=-2).reshape(x.shape)
    return (x * cos + rot * sin).astype(x.dtype)


def short_conv(z, w):
    zp = jnp.pad(z, ((0, 0), (1, 1), (0, 0)))
    return zp[:, :-2] * w[0] + zp[:, 1:-1] * w[1] + zp[:, 2:] * w[2]


def multiscale_pool(z, w_pool, scale):
    B, L, C = z.shape
    zf = z.astype(jnp.float32)
    cs = jnp.pad(jnp.cumsum(zf, axis=1), ((0, 0), (1, 0), (0, 0))).reshape(B, L + 1, POOL_GROUPS, POOL_CH)
    t = jnp.arange(L)
    means = []
    for g, w in enumerate(POOL_WINDOWS):
        lo = jnp.clip(t - w // 2, 0, L)
        hi = jnp.clip(t + w // 2, 0, L)
        cg = cs[:, :, g]
        means.append((cg[:, hi] - cg[:, lo]) / (hi - lo).astype(jnp.float32)[None, :, None])
    pooled = jnp.stack(means, axis=2) - zf.reshape(B, L, POOL_GROUPS, POOL_CH)
    y = jnp.einsum('blgc,gcd->blgd', pooled.astype(z.dtype), w_pool).reshape(B, L, C)
    return y * scale


def mla_expand(ckv, w_uk, w_uv):
    k_nope = jnp.einsum('blr,rhd->blhd', ckv, w_uk)
    v = jnp.einsum('blr,rhd->blhd', ckv, w_uv)
    return k_nope, v


def mla_attend(q_nope, q_rope, k_nope, k_rope, v):
    B, Lq, H, N = q_nope.shape
    nblk = Lq // Q_BLOCK
    sm_scale = (MLA_NOPE + MLA_ROPE) ** -0.5

    def one_block(blk):
        qn, qr = blk
        s = jnp.einsum('bqhn,bkhn->bhqk', qn, k_nope) + jnp.einsum('bqhr,bkr->bhqk', qr, k_rope)
        p = jax.nn.softmax(s.astype(jnp.float32) * sm_scale, axis=-1)
        return jnp.einsum('bhqk,bkhv->bqhv', p.astype(v.dtype), v)

    qn_b = q_nope.reshape(B, nblk, Q_BLOCK, H, N).swapaxes(0, 1)
    qr_b = q_rope.reshape(B, nblk, Q_BLOCK, H, MLA_ROPE).swapaxes(0, 1)
    out = lax.map(one_block, (qn_b, qr_b))
    return out.swapaxes(0, 1).reshape(B, Lq, H * MLA_V)


def retention_chunks(q, k, v, log_gamma, s0):
    B, L, H, DK = q.shape
    DV = v.shape[-1]
    n = L // RET_CHUNK

    def chunked(a):
        return a.reshape(B, n, RET_CHUNK, H, a.shape[-1]).transpose(1, 0, 3, 2, 4)

    pos = jnp.arange(RET_CHUNK, dtype=jnp.float32)
    rel = pos[:, None] - pos[None, :]
    dmat = jnp.where(rel >= 0, jnp.exp(log_gamma[:, None, None] * jnp.maximum(rel, 0.0)), 0.0)
    dq = jnp.exp(log_gamma[:, None] * (pos + 1.0))[..., None]
    dk = jnp.exp(log_gamma[:, None] * (RET_CHUNK - 1.0 - pos))[..., None]
    dc = jnp.exp(log_gamma * RET_CHUNK)[:, None, None]

    def step(S, qkv):
        qi, ki, vi = qkv
        inner = jnp.einsum('bhqd,bhkd->bhqk', qi, ki) * dmat
        o = jnp.einsum('bhqk,bhkv->bhqv', inner, vi) + jnp.einsum('bhqd,bhdv->bhqv', qi, S) * dq
        S = S * dc + jnp.einsum('bhkd,bhkv->bhdv', ki * dk, vi)
        return S, o

    S, o = lax.scan(step, s0, (chunked(q), chunked(k), chunked(v)))
    return o.transpose(1, 0, 3, 2, 4).reshape(B, L, H, DV), S


def retention_bidir(q, k, v, lg_f, lg_b, s_f, s_b):
    flip = lambda a: jnp.flip(a, axis=1)
    o_f, S_f = retention_chunks(q, k, v, lg_f, s_f)
    o_b, S_b = retention_chunks(flip(q), flip(k), flip(v), lg_b, s_b)
    return o_f + flip(o_b), S_f, S_b


def head_groupnorm(o):
    mu = jnp.mean(o, axis=-1, keepdims=True)
    var = jnp.mean(jnp.square(o - mu), axis=-1, keepdims=True)
    return (o - mu) * lax.rsqrt(var + EPS)


def token_mix(h, P, l, ctx):
    B, L, _ = h.shape
    dt = h.dtype
    z = h @ P['w_in'][l]

    a_b, a_c, a_h = jnp.split(z[..., :OFF_MLA], 3, axis=-1)
    y_a = a_b * short_conv(a_c * a_h, P['conv_w'][l])

    zm = z[..., OFF_MLA:OFF_RET]
    q_lat = zm[..., :Q_LORA]
    ckv = rmsnorm(zm[..., Q_LORA:Q_LORA + KV_LORA], P['mla_kv_norm'][l])
    k_rope = zm[..., Q_LORA + KV_LORA:]
    q = jnp.einsum('blr,rhd->blhd', rmsnorm(q_lat, P['mla_q_norm'][l]), P['mla_w_uq'][l])
    q_nope, q_rope = q[..., :MLA_NOPE], q[..., MLA_NOPE:]
    k_nope, v = mla_expand(ckv, P['mla_w_uk'][l], P['mla_w_uv'][l])

    zr = z[..., OFF_RET:OFF_POOL].reshape(B, L, 4, RET_HEADS, RET_DK).astype(jnp.float32)
    rq, rk, rv, rg = zr[:, :, 0], zr[:, :, 1] * (RET_DK ** -0.5), zr[:, :, 2], zr[:, :, 3]
    lg = jax.nn.log_sigmoid(P['ret_decay_logit'][l].astype(jnp.float32))

    y_p = multiscale_pool(z[..., OFF_POOL:], P['pool_w'][l], P['pool_scale'][l])

    if ctx is None:
        y_m = mla_attend(q_nope, q_rope, k_nope, k_rope, v)
        zero = jnp.zeros((B, RET_HEADS, RET_DK, RET_DV), jnp.float32)
        o_r, S_f, S_b = retention_bidir(rq, rk, rv, lg[0], lg[1], zero, zero)
        new = (ckv, k_rope, jnp.stack([S_f, S_b], axis=1).astype(dt))
    else:
        ckv_c, krope_c, st_c = ctx
        cos, sin = rope_tables(L)
        q_rope = axial_rotate(q_rope, cos[:, None, :], sin[:, None, :])
        k_rope = axial_rotate(k_rope, cos, sin)
        kn_c, v_c = mla_expand(ckv_c, P['mla_w_uk'][l], P['mla_w_uv'][l])
        y_m = mla_attend(q_nope, q_rope,
                         jnp.concatenate([kn_c, k_nope], axis=1),
                         jnp.concatenate([krope_c, k_rope], axis=1),
                         jnp.concatenate([v_c, v], axis=1))
        st_c = st_c.astype(jnp.float32)
        o_r, _, _ = retention_bidir(rq, rk, rv, lg[0], lg[1], st_c[:, 0], st_c[:, 1])
        new = None
    y_r = (head_groupnorm(o_r) * jax.nn.silu(rg)).astype(dt).reshape(B, L, GROUP_W)

    y = jnp.concatenate([y_a, y_m.astype(dt), y_r, y_p], axis=-1)
    return y @ P['w_out'][l], new


def peer(h, w_q, subkeys, u_tab, v_tab):
    B, L, D = h.shape
    T = B * L
    xt = h.reshape(T, D)
    q = (xt @ w_q).reshape(T, PEER_HEADS, 2, PEER_HALF)
    s = jnp.einsum('thpd,hpnd->thpn', q, subkeys)
    s1, i1 = lax.top_k(s[:, :, 0], PEER_TOPK)
    s2, i2 = lax.top_k(s[:, :, 1], PEER_TOPK)
    cand = (s1[..., :, None] + s2[..., None, :]).reshape(T, PEER_HEADS, PEER_TOPK * PEER_TOPK)
    cidx = (i1[..., :, None] * PEER_N_KEYS + i2[..., None, :]).reshape(T, PEER_HEADS, PEER_TOPK * PEER_TOPK)
    top_s, sel = lax.top_k(cand, PEER_TOPK)
    eidx = jnp.take_along_axis(cidx, sel, axis=-1)
    gates = jax.nn.softmax(top_s.astype(jnp.float32), axis=-1)
    nb = T // TOKEN_BLOCK

    def block(args):
        xb, eb, gb = args
        a = jnp.einsum('td,thkd->thk', xb, u_tab[eb])
        coef = gb.astype(xb.dtype) * jax.nn.gelu(a)
        return jnp.einsum('thk,thkd->td', coef, v_tab[eb])

    y = lax.map(block, (xt.reshape(nb, TOKEN_BLOCK, D),
                        eidx.reshape(nb, TOKEN_BLOCK, PEER_HEADS, PEER_TOPK),
                        gates.reshape(nb, TOKEN_BLOCK, PEER_HEADS, PEER_TOPK)))
    return y.reshape(B, L, D)


def trunk_layer(x, mods, P, l, ctx):
    sh1, sc1, g1, sh2, sc2, g2 = mods
    h = rmsnorm(x, P['norm1'][l]) * (1 + sc1) + sh1
    mix, new = token_mix(h, P, l, ctx)
    x = x + g1 * mix
    h = rmsnorm(x, P['norm2'][l]) * (1 + sc2) + sh2
    x = x + g2 * peer(h, P['peer_w_q'][l], P['peer_subkeys'][l], P['peer_u'][l], P['peer_v'][l])
    return x, new


def setup_inputs(seed: int = 0) -> dict:
    key = jax.random.key(seed)
    keys = iter(jax.random.split(key, 40))
    f32 = jnp.float32

    def nrm(shape, scale):
        return jax.random.normal(next(keys), shape, f32) * scale

    gamma = 1.0 - 2.0 ** (-5.0 - jnp.arange(RET_HEADS, dtype=f32))
    base_logit = jnp.log(gamma) - jnp.log1p(-gamma)
    return {
        'x_prompt': nrm((BATCH, SEQ, D_MODEL), 1.0),
        'x_sample': nrm((DEC_BATCH, DEC_SEQ, D_MODEL), 1.0),
        'cache_mla_ckv': nrm((DEC_BATCH, DEPTH, PAST_LEN, KV_LORA), 1.0),
        'cache_mla_krope': nrm((DEC_BATCH, DEPTH, PAST_LEN, MLA_ROPE), 1.0),
        'state_ret': nrm((DEC_BATCH, DEPTH, 2, RET_HEADS, RET_DK, RET_DV), 0.5),
        'c': nrm((DEC_BATCH, D_MODEL), 1.0),
        'c_ctx': nrm((D_MODEL,), 1.0),
        'w_mod': nrm((DEPTH, D_MODEL, 6 * D_MODEL), 0.5 * D_MODEL ** -0.5),
        'b_mod': nrm((DEPTH, 6 * D_MODEL), 0.02),
        'norm1': 1.0 + nrm((DEPTH, D_MODEL), 0.1),
        'norm2': 1.0 + nrm((DEPTH, D_MODEL), 0.1),
        'w_in': nrm((DEPTH, D_MODEL, IN_COLS), D_MODEL ** -0.5),
        'conv_w': nrm((DEPTH, 3, CONV_W), 3 ** -0.5),
        'mla_q_norm': 1.0 + nrm((DEPTH, Q_LORA), 0.1),
        'mla_w_uq': nrm((DEPTH, Q_LORA, MLA_HEADS, MLA_NOPE + MLA_ROPE), Q_LORA ** -0.5),
        'mla_kv_norm': 1.0 + nrm((DEPTH, KV_LORA), 0.1),
        'mla_w_uk': nrm((DEPTH, KV_LORA, MLA_HEADS, MLA_NOPE), KV_LORA ** -0.5),
        'mla_w_uv': nrm((DEPTH, KV_LORA, MLA_HEADS, MLA_V), KV_LORA ** -0.5),
        'ret_decay_logit': base_logit + nrm((DEPTH, 2, RET_HEADS), 0.1),
        'pool_w': nrm((DEPTH, POOL_GROUPS, POOL_CH, POOL_CH), POOL_CH ** -0.5),
        'pool_scale': 1.0 + nrm((DEPTH, GROUP_W), 0.1),
        'w_out': nrm((DEPTH, MIX_W, D_MODEL), MIX_W ** -0.5),
        'peer_w_q': nrm((DEPTH, D_MODEL, PEER_HEADS * PEER_QDIM), D_MODEL ** -0.5),
        'peer_subkeys': nrm((DEPTH, PEER_HEADS, 2, PEER_N_KEYS, PEER_HALF), PEER_HALF ** -0.5),
        'peer_u': nrm((DEPTH, PEER_EXPERTS, D_MODEL), D_MODEL ** -0.5),
        'peer_v': nrm((DEPTH, PEER_EXPERTS, D_MODEL), PEER_HEADS ** -0.5),
        'norm_f': 1.0 + nrm((D_MODEL,), 0.1),
    }


def reference(x_prompt, x_sample, cache_mla_ckv, cache_mla_krope, state_ret, c, c_ctx,
              w_mod, b_mod, norm1, norm2, w_in, conv_w, mla_q_norm, mla_w_uq, mla_kv_norm,
              mla_w_uk, mla_w_uv, ret_decay_logit, pool_w, pool_scale, w_out, peer_w_q,
              peer_subkeys, peer_u, peer_v, norm_f):
    P = dict(norm1=norm1, norm2=norm2, w_in=w_in, conv_w=conv_w, mla_q_norm=mla_q_norm,
             mla_w_uq=mla_w_uq, mla_kv_norm=mla_kv_norm, mla_w_uk=mla_w_uk, mla_w_uv=mla_w_uv,
             ret_decay_logit=ret_decay_logit, pool_w=pool_w, pool_scale=pool_scale, w_out=w_out,
             peer_w_q=peer_w_q, peer_subkeys=peer_subkeys, peer_u=peer_u, peer_v=peer_v)

    xp = x_prompt
    ckv_list, krope_list, ret_list = [], [], []
    for l in range(DEPTH):
        mods = modulation(c_ctx, w_mod[l], b_mod[l])
        xp, (ckv_l, krope_l, ret_l) = trunk_layer(xp, mods, P, l, None)
        ckv_list.append(ckv_l)
        krope_list.append(krope_l)
        ret_list.append(ret_l)
    y_prompt = rmsnorm(xp, norm_f)
    new_mla_ckv = jnp.stack(ckv_list, axis=1)
    new_mla_krope = jnp.stack(krope_list, axis=1)
    new_state_ret = jnp.stack(ret_list, axis=1)

    xs = x_sample
    for l in range(DEPTH):
        mods = modulation(c, w_mod[l], b_mod[l])
        ctx = (cache_mla_ckv[:, l], cache_mla_krope[:, l], state_ret[:, l])
        xs, _ = trunk_layer(xs, mods, P, l, ctx)
    y_sample = rmsnorm(xs, norm_f)

    return (y_prompt, y_sample, new_mla_ckv, new_mla_krope, new_state_ret)
```

```python
import functools

import jax
import jax.numpy as jnp
import numpy as np
from jax import lax
from jax.experimental import pallas as pl
from jax.experimental.pallas import tpu as pltpu

F32 = jnp.float32
BF16 = jnp.bfloat16

D = 1024
N_PROMPT_SEQ, L_PROMPT = 16, 256
N_SAMPLE_SEQ, L_SAMPLE = 2, 1024
PAST = 256
DEPTH = 2
T_PROMPT = N_PROMPT_SEQ * L_PROMPT
T_SAMPLE = N_SAMPLE_SEQ * L_SAMPLE
T_ALL = T_PROMPT + T_SAMPLE
GRID_W = 64
GW = 256
HD = 64
N_HEADS = 4
ROPE = 32
Q_LORA, KV_LORA = 256, 128
EPS = 1e-6
ROPE_BASE = 10000.0
SM_SCALE = (HD + ROPE) ** -0.5
RET_SCALE = HD ** -0.5
POOL_HALF = (1, 2, 4, 8)
PEER_H, PEER_KEYS, PEER_K = 8, 128, 16
N_EXPERTS = PEER_KEYS * PEER_KEYS
LANES = 128
ROWS = 256
PAD = 8

C_CONV, C_QLAT, C_CKV, C_RET, C_POOL, C_KR, C_KRR, C_END = 0, 768, 1024, 1152, 2176, 2432, 2560, 2688
Y_A, Y_M, Y_R, Y_P, Y_END = 0, 256, 768, 1024, 1280

VMEM_LIMIT = 56 * 1024 * 1024


def _dot(a, b):
    return jnp.dot(a, b, preferred_element_type=F32)


def _dot_nt(a, b):
    return lax.dot_general(a, b, (((1,), (1,)), ((), ())), preferred_element_type=F32)


def _dot_tn(a, b):
    return lax.dot_general(a, b, (((0,), (0,)), ((), ())), preferred_element_type=F32)


def _rms(x, g):
    return x * lax.rsqrt(jnp.mean(x * x, axis=-1, keepdims=True) + EPS) * g


def _silu(x):
    return x * (1.0 / (1.0 + jnp.exp(-x)))


def _gelu_tanh(x):
    return 0.5 * x * (1.0 + jnp.tanh(np.sqrt(2.0 / np.pi).astype(np.float32) * (x + 0.044715 * (x * x * x))))


def _log_sigmoid(x):
    return jnp.minimum(x, 0.0) - jnp.log(1.0 + jnp.exp(-jnp.abs(x)))


def _dot_split(a, p):
    hi = a.astype(BF16)
    lo = (a - hi.astype(F32)).astype(BF16)
    return _dot(hi, p) + _dot(lo, p)


def _mods_kernel(c_ref, w_ref, b_ref, o_ref):
    s = _silu(c_ref[...]).astype(BF16)
    o_ref[...] = _dot(s, w_ref[...].astype(BF16)) + b_ref[...]


def _modulation(c8, w_mod, b_mod):
    nb = 6 * D // D
    return pl.pallas_call(
        _mods_kernel,
        out_shape=jax.ShapeDtypeStruct((DEPTH, 8, 6 * D), F32),
        grid=(DEPTH, nb),
        in_specs=[
            pl.BlockSpec((8, D), lambda l, j: (0, 0)),
            pl.BlockSpec((None, D, D), lambda l, j: (l, 0, j)),
            pl.BlockSpec((None, 1, D), lambda l, j: (l, 0, j)),
        ],
        out_specs=pl.BlockSpec((None, 8, D), lambda l, j: (l, 0, j)),
        name="adaln_modulation",
    )(c8, w_mod, b_mod.reshape(DEPTH, 1, 6 * D))


def _mix_kernel(L, has_ctx, *refs):
    it = iter(refs)
    x_ref, mod_ref, n1_ref, win_ref, conv_ref, qn_ref, wuq_ref = (next(it) for _ in range(7))
    wuqr_ref = next(it) if has_ctx else None
    kvn_ref, wuk_ref, wuv_ref, lg_ref, poolw_ref, pools_ref, wout_ref = (next(it) for _ in range(7))
    if has_ctx:
        cos_ref, sin_ref, ckvc_ref, krc_ref, st_ref = (next(it) for _ in range(5))
    o_ref = next(it)
    if not has_ctx:
        ckv_o, kr_o, st_o = (next(it) for _ in range(3))
    (u_s, zp_s, ab_s, q_s, k_s, v_s, rq_s, rk_s, rv_s, rg_s, or_s, ycat_s) = (next(it) for _ in range(12))

    LK = L + (PAST if has_ctx else 0)
    K0 = PAST if has_ctx else 0
    nblk = L // ROWS

    m = mod_ref[...]
    sh1, sc1, g1 = m[:, 0:D], m[:, D:2 * D], m[:, 2 * D:3 * D]

    u_s[0:PAD, :] = jnp.zeros((PAD, GW), F32)
    u_s[PAD + L:PAD + L + PAD, :] = jnp.zeros((PAD, GW), F32)
    zp_s[0:PAD, :] = jnp.zeros((PAD, GW), F32)
    zp_s[PAD + L:PAD + L + PAD, :] = jnp.zeros((PAD, GW), F32)

    if has_ctx:
        cc = ckvc_ref[...].astype(BF16)
        krc = krc_ref[...]
        knc = _dot(cc, wuk_ref[...])
        for h in range(N_HEADS):
            k_s[0:PAST, h * LANES:(h + 1) * LANES] = (knc[:, h * LANES:(h + 1) * LANES] + krc).astype(BF16)
        v_s[0:PAST, :] = _dot(cc, wuv_ref[...]).astype(BF16)

    def phase_a(rb, carry):
        r0 = pl.multiple_of(rb * ROWS, ROWS)
        x = x_ref[pl.ds(r0, ROWS), :]
        hb = (_rms(x, n1_ref[...]) * (1.0 + sc1) + sh1).astype(BF16)

        zc = _dot(hb, win_ref[:, C_CONV:C_QLAT])
        ab_s[pl.ds(r0, ROWS), :] = zc[:, 0:GW]
        u_s[pl.ds(PAD + r0, ROWS), :] = zc[:, GW:2 * GW] * zc[:, 2 * GW:3 * GW]

        qn = _rms(_dot(hb, win_ref[:, C_QLAT:C_CKV]), qn_ref[...]).astype(BF16)
        q = _dot(qn, wuq_ref[...])
        ckv = _rms(_dot(hb, win_ref[:, C_CKV:C_RET]), kvn_ref[...])
        kr = _dot(hb, win_ref[:, C_KR:C_KRR])
        if has_ctx:
            cos = cos_ref[pl.ds(r0, ROWS), :]
            sin = sin_ref[pl.ds(r0, ROWS), :]
            qr = _dot(qn, wuqr_ref[...])
            q = jnp.concatenate(
                [q[:, h * LANES:(h + 1) * LANES] * cos + qr[:, h * LANES:(h + 1) * LANES] * sin
                 for h in range(N_HEADS)], axis=1)
            kr = kr * cos + _dot(hb, win_ref[:, C_KRR:C_END]) * sin
        else:
            ckv_o[pl.ds(r0, ROWS), :] = ckv
            kr_o[pl.ds(r0, ROWS), :] = kr[:, 0:ROPE]
        q_s[pl.ds(r0, ROWS), :] = q.astype(BF16)
        ckvb = ckv.astype(BF16)
        kn = _dot(ckvb, wuk_ref[...])
        for h in range(N_HEADS):
            k_s[pl.ds(K0 + r0, ROWS), h * LANES:(h + 1) * LANES] = (kn[:, h * LANES:(h + 1) * LANES] + kr).astype(BF16)
        v_s[pl.ds(K0 + r0, ROWS), :] = _dot(ckvb, wuv_ref[...]).astype(BF16)

        zr = _dot(hb, win_ref[:, C_RET:C_POOL])
        rq_s[pl.ds(r0, ROWS), :] = zr[:, 0:GW].astype(BF16)
        rk_s[pl.ds(r0, ROWS), :] = (zr[:, GW:2 * GW] * RET_SCALE).astype(BF16)
        rv_s[pl.ds(r0, ROWS), :] = zr[:, 2 * GW:3 * GW].astype(BF16)
        rg_s[pl.ds(r0, ROWS), :] = zr[:, 3 * GW:4 * GW]

        zp_s[pl.ds(PAD + r0, ROWS), :] = _dot(hb, win_ref[:, C_POOL:C_KR])
        return carry

    lax.fori_loop(0, nblk, phase_a, 0)

    lane_g = lax.broadcasted_iota(jnp.int32, (ROWS, GW), 1)
    row_g = lax.broadcasted_iota(jnp.int32, (ROWS, GW), 0)
    half_w = jnp.where(lane_g < HD, POOL_HALF[0],
                       jnp.where(lane_g < 2 * HD, POOL_HALF[1],
                                 jnp.where(lane_g < 3 * HD, POOL_HALF[2], POOL_HALF[3])))
    cw = conv_ref[...]
    for rb in range(nblk):
        r0 = rb * ROWS
        u_prev = u_s[PAD + r0 - 1:PAD + r0 - 1 + ROWS, :]
        u_mid = u_s[PAD + r0:PAD + r0 + ROWS, :]
        u_next = u_s[PAD + r0 + 1:PAD + r0 + 1 + ROWS, :]
        y_a = ab_s[r0:r0 + ROWS, :] * (u_prev * cw[0:1, :] + u_mid * cw[1:2, :] + u_next * cw[2:3, :])
        ycat_s[r0:r0 + ROWS, Y_A:Y_M] = y_a.astype(BF16)

        zmid = zp_s[PAD + r0:PAD + r0 + ROWS, :]
        acc = jnp.zeros((ROWS, GW), F32)
        for d in range(-PAD, PAD):
            zs = zmid if d == 0 else zp_s[PAD + r0 + d:PAD + r0 + d + ROWS, :]
            lo = -d if d < 0 else d + 1
            acc = acc + jnp.where(half_w >= lo, zs, 0.0)
        t = row_g + r0
        cnt = (jnp.minimum(t + half_w, L) - jnp.maximum(t - half_w, 0)).astype(F32)
        pooled = (acc / cnt - zmid).astype(BF16)
        y_p = _dot(pooled, poolw_ref[...]) * pools_ref[...]
        ycat_s[r0:r0 + ROWS, Y_P:Y_END] = y_p.astype(BF16)

    def phase_c(qb, carry):
        r0 = pl.multiple_of(qb * ROWS, ROWS)
        for h in range(N_HEADS):
            qh = q_s[pl.ds(r0, ROWS), h * LANES:(h + 1) * LANES]
            s = _dot_nt(qh, k_s[:, h * LANES:(h + 1) * LANES]) * SM_SCALE
            s = s - jnp.max(s, axis=-1, keepdims=True)
            p = jnp.exp(s)
            p = p / jnp.sum(p, axis=-1, keepdims=True)
            o = _dot(p.astype(BF16), v_s[:, h * LANES:(h + 1) * LANES])
            ycat_s[pl.ds(r0, ROWS), Y_M + h * LANES:Y_M + (h + 1) * LANES] = o.astype(BF16)
        return carry

    lax.fori_loop(0, nblk, phase_c, 0)

    lg = _log_sigmoid(lg_ref[...])
    lane_s = lax.broadcasted_iota(jnp.int32, (1, LANES), 1)
    first_half = lane_s < HD

    def slot_decay(direction, j):
        a = lg[direction * N_HEADS + 2 * j:direction * N_HEADS + 2 * j + 1, 0:LANES]
        b = lg[direction * N_HEADS + 2 * j + 1:direction * N_HEADS + 2 * j + 2, 0:LANES]
        return jnp.where(first_half, a, b)

    def phase_d(qb, carry):
        r0 = pl.multiple_of(qb * ROWS, ROWS)
        t_i = lax.broadcasted_iota(jnp.int32, (ROWS, L), 0) + r0
        s_i = lax.broadcasted_iota(jnp.int32, (ROWS, L), 1)
        rel = (t_i - s_i).astype(F32)
        fwd = jnp.maximum(rel, 0.0)
        bwd = jnp.maximum(-rel, 0.0)
        for j in range(N_HEADS // 2):
            q_slot = rq_s[pl.ds(r0, ROWS), j * LANES:(j + 1) * LANES]
            k_slot = rk_s[:, j * LANES:(j + 1) * LANES]
            v_slot = rv_s[:, j * LANES:(j + 1) * LANES]
            outs = []
            for half in range(2):
                h = 2 * j + half
                keep = first_half if half == 0 else jnp.logical_not(first_half)
                qh = jnp.where(keep, q_slot, jnp.zeros_like(q_slot))
                sc = _dot_nt(qh, k_slot)
                lgf = lg[h:h + 1, :]
                lgb = lg[N_HEADS + h:N_HEADS + h + 1, :]
                mask = (jnp.where(rel >= 0, jnp.exp(lgf * fwd), 0.0)
                        + jnp.where(rel <= 0, jnp.exp(lgb * bwd), 0.0))
                outs.append(_dot((sc * mask).astype(BF16), v_slot))
            o = jnp.where(first_half, outs[0], outs[1])
            if has_ctx:
                tcol = (lax.broadcasted_iota(jnp.int32, (ROWS, LANES), 0) + r0).astype(F32)
                o = o + _dot(q_slot, st_ref[0, j].astype(BF16)) * jnp.exp(slot_decay(0, j) * (tcol + 1.0))
                o = o + _dot(q_slot, st_ref[1, j].astype(BF16)) * jnp.exp(slot_decay(1, j) * (float(L) - tcol))
            or_s[pl.ds(r0, ROWS), j * LANES:(j + 1) * LANES] = o
        return carry

    lax.fori_loop(0, nblk, phase_d, 0)

    if not has_ctx:
        tcol = lax.broadcasted_iota(jnp.int32, (L, LANES), 0).astype(F32)
        for j in range(N_HEADS // 2):
            k_slot = rk_s[:, j * LANES:(j + 1) * LANES].astype(F32)
            v_slot = rv_s[:, j * LANES:(j + 1) * LANES]
            kf = (k_slot * jnp.exp(slot_decay(0, j) * (float(L - 1) - tcol))).astype(BF16)
            kb = (k_slot * jnp.exp(slot_decay(1, j) * tcol)).astype(BF16)
            st_o[0, j] = _dot_tn(kf, v_slot)
            st_o[1, j] = _dot_tn(kb, v_slot)

    pr = jnp.right_shift(lax.broadcasted_iota(jnp.int32, (GW, GW), 0), 6)
    pc = jnp.right_shift(lax.broadcasted_iota(jnp.int32, (GW, GW), 1), 6)
    pavg = jnp.where(pr == pc, 1.0 / HD, 0.0).astype(BF16)

    def phase_e(rb, carry):
        r0 = pl.multiple_of(rb * ROWS, ROWS)
        o = or_s[pl.ds(r0, ROWS), :]
        dlt = o - _dot_split(o, pavg)
        var = _dot_split(dlt * dlt, pavg)
        y_r = dlt * lax.rsqrt(var + EPS) * _silu(rg_s[pl.ds(r0, ROWS), :])
        ycat_s[pl.ds(r0, ROWS), Y_R:Y_P] = y_r.astype(BF16)
        mix = _dot(ycat_s[pl.ds(r0, ROWS), :], wout_ref[...])
        o_ref[pl.ds(r0, ROWS), :] = x_ref[pl.ds(r0, ROWS), :] + g1 * mix
        return carry

    lax.fori_loop(0, nblk, phase_e, 0)


def _token_mix(xall, mods4, layer, W, L, n_seq, blk_off, mod_row, ctx):
    has_ctx = ctx is not None
    LK = L + (PAST if has_ctx else 0)
    full = lambda shape: pl.BlockSpec(shape, lambda b: (0,) * len(shape))
    ins = [xall, mods4, W["norm1"], W["w_in"], W["conv_w"], W["q_norm"], W["w_uq"]]
    specs = [
        pl.BlockSpec((L, D), lambda b: (blk_off + b, 0)),
        pl.BlockSpec((None, None, 1, 6 * D), lambda b: (layer, mod_row(b), 0, 0)),
        full((1, D)), full((D, C_END)), full((3, GW)), full((1, Q_LORA)), full((Q_LORA, 4 * LANES)),
    ]
    if has_ctx:
        ins.append(W["w_uq_rot"]); specs.append(full((Q_LORA, 4 * LANES)))
    ins += [W["kv_norm"], W["w_uk"], W["w_uv"], W["lg"], W["pool_w"], W["pool_scale"], W["w_out"]]
    specs += [full((1, KV_LORA)), full((KV_LORA, 4 * LANES)), full((KV_LORA, 4 * LANES)),
              pl.BlockSpec((8, L), lambda b: (0, 0)), full((GW, GW)), full((1, GW)), full((Y_END, D))]
    if has_ctx:
        cos, sin, ckvc, krc, st = ctx
        ins += [cos, sin, ckvc, krc, st]
        specs += [full((L, LANES)), full((L, LANES)),
                  pl.BlockSpec((None, PAST, KV_LORA), lambda b: (b, 0, 0)),
                  pl.BlockSpec((None, PAST, LANES), lambda b: (b, 0, 0)),
                  pl.BlockSpec((None, 2, 2, LANES, LANES), lambda b: (b, 0, 0, 0, 0))]
    out_shape = [jax.ShapeDtypeStruct((T_ALL, D), F32)]
    out_specs = [pl.BlockSpec((L, D), lambda b: (blk_off + b, 0))]
    if not has_ctx:
        out_shape += [jax.ShapeDtypeStruct((n_seq, L, KV_LORA), F32),
                      jax.ShapeDtypeStruct((n_seq, L, ROPE), F32),
                      jax.ShapeDtypeStruct((n_seq, 2, 2, LANES, LANES), F32)]
        out_specs += [pl.BlockSpec((None, L, KV_LORA), lambda b: (b, 0, 0)),
                      pl.BlockSpec((None, L, ROPE), lambda b: (b, 0, 0)),
                      pl.BlockSpec((None, 2, 2, LANES, LANES), lambda b: (b, 0, 0, 0, 0))]
    scratch = [
        pltpu.VMEM((L + 2 * PAD, GW), F32),
        pltpu.VMEM((L + 2 * PAD, GW), F32),
        pltpu.VMEM((L, GW), F32),
        pltpu.VMEM((L, 4 * LANES), BF16),
        pltpu.VMEM((LK, 4 * LANES), BF16),
        pltpu.VMEM((LK, 4 * LANES), BF16),
        pltpu.VMEM((L, GW), BF16),
        pltpu.VMEM((L, GW), BF16),
        pltpu.VMEM((L, GW), BF16),
        pltpu.VMEM((L, GW), F32),
        pltpu.VMEM((L, GW), F32),
        pltpu.VMEM((L, Y_END), BF16),
    ]
    outs = pl.pallas_call(
        functools.partial(_mix_kernel, L, has_ctx),
        out_shape=out_shape,
        grid=(n_seq,),
        in_specs=specs,
        out_specs=out_specs,
        scratch_shapes=scratch,
        input_output_aliases={0: 0},
        compiler_params=pltpu.CompilerParams(dimension_semantics=("arbitrary",), vmem_limit_bytes=VMEM_LIMIT),
        name="token_mix_ctx" if has_ctx else "token_mix_prompt",
    )(*ins)
    return outs


TB1 = 256
CH = 128
NOT_SEL = 99.0


def _top16(s):
    rows = lax.broadcasted_iota(jnp.int32, s.shape, 0).astype(F32)
    rows16 = lax.broadcasted_iota(jnp.int32, (PEER_K, s.shape[1]), 0)
    rank = jnp.full(s.shape, NOT_SEL, F32)
    vals = []
    vals16 = jnp.zeros((PEER_K, s.shape[1]), F32)
    for a in range(PEER_K):
        mx = jnp.max(s, axis=0, keepdims=True)
        idx = jnp.min(jnp.where(s == mx, rows, float(PEER_KEYS)), axis=0, keepdims=True)
        sel = rows == idx
        rank = jnp.where(sel, float(a), rank)
        s = jnp.where(sel, -jnp.inf, s)
        vals.append(mx)
        vals16 = jnp.where(rows16 == a, mx, vals16)
    return rank, vals, vals16


def _route_kernel(x_ref, mod_ref, n2_ref, wq_ref, sk1_ref, sk2_ref,
                  h_o, n_o, e1_o, b_o, e2_o, s1_s, s2_s):
    m = mod_ref[...]
    sh2, sc2 = m[:, 3 * D:4 * D], m[:, 4 * D:5 * D]
    hb = (_rms(x_ref[...], n2_ref[...]) * (1.0 + sc2) + sh2).astype(BF16)
    h_o[...] = hb
    q = _dot(hb, wq_ref[...]).astype(BF16)
    for h in range(PEER_H):
        qs = q[:, h * LANES:(h + 1) * LANES]
        s1_s[h] = _dot_nt(sk1_ref[h], qs)
        s2_s[h] = _dot_nt(sk2_ref[h], qs)

    sub = lax.broadcasted_iota(jnp.int32, (8, CH), 0).astype(F32)

    def per_head(h, carry):
        for c in range(TB1 // CH):
            cs = slice(c * CH, (c + 1) * CH)
            s1 = s1_s[h, :, cs]
            s2 = s2_s[h, :, cs]
            ra, v1, v1all = _top16(s1)
            rb, v2, v2all = _top16(s2)
            v2lo = v2all[0:8]
            v2hi = v2all[8:16]
            v1hi = v1all[8:16]
            tiles = [v1[0] + v2lo, v1[0] + v2hi]
            flats = [sub, sub + 8.0]
            for a in range(1, 8):
                tiles.append(v1[a] + v2lo)
                flats.append(sub + 16.0 * a)
            tiles.append(v1hi + v2[0])
            flats.append((sub + 8.0) * 16.0)
            cand = jnp.concatenate(tiles, axis=0)
            flat = jnp.concatenate(flats, axis=0)
            selm = jnp.zeros(cand.shape, F32)
            top = v1[0] + v2[0]
            z = jnp.zeros((1, CH), F32)
            for _ in range(PEER_K):
                mx = jnp.max(cand, axis=0, keepdims=True)
                idx = jnp.min(jnp.where(cand == mx, flat, 1e9), axis=0, keepdims=True)
                sel = flat == idx
                selm = jnp.where(sel, 1.0, selm)
                cand = jnp.where(sel, -jnp.inf, cand)
                z = z + jnp.exp(mx - top)
            n_rows = [jnp.sum(selm[0:16], axis=0, keepdims=True)]
            for a in range(1, 8):
                n_rows.append(jnp.sum(selm[8 * (a + 1):8 * (a + 2)], axis=0, keepdims=True))
            for r in range(8):
                n_rows.append(selm[72 + r:73 + r])
            nsel = jnp.zeros(ra.shape, F32)
            for a in range(PEER_K):
                nsel = jnp.where(ra == float(a), n_rows[a], nsel)
            n_o[h, :, cs] = nsel
            e1_o[h, :, cs] = jnp.exp(s1 - v1[0]) * (1.0 / z)
            b_o[h, :, cs] = rb
            e2_o[h, :, cs] = jnp.exp(s2 - v2[0])
        return carry

    lax.fori_loop(0, PEER_H, per_head, 0)


def _peer_route(xall, mods4, layer, W):
    nblk = T_ALL // TB1

    def mod_row(i):
        start = T_PROMPT // TB1
        per = L_SAMPLE // TB1
        return jnp.where(i < start, 0, 1 + (i - start) // per)

    full = lambda shape: pl.BlockSpec(shape, lambda i: (0,) * len(shape))
    kt = jax.ShapeDtypeStruct((PEER_H, PEER_KEYS, T_ALL), F32)
    kt_spec = pl.BlockSpec((PEER_H, PEER_KEYS, TB1), lambda i: (0, 0, i))
    return pl.pallas_call(
        _route_kernel,
        out_shape=[jax.ShapeDtypeStruct((T_ALL, D), BF16), kt, kt, kt, kt],
        grid=(nblk,),
        in_specs=[
            pl.BlockSpec((TB1, D), lambda i: (i, 0)),
            pl.BlockSpec((None, None, 1, 6 * D), lambda i: (layer, mod_row(i), 0, 0)),
            full((1, D)), full((D, D)),
            full((PEER_H, PEER_KEYS, LANES)), full((PEER_H, PEER_KEYS, LANES)),
        ],
        out_specs=[pl.BlockSpec((TB1, D), lambda i: (i, 0)), kt_spec, kt_spec, kt_spec, kt_spec],
        scratch_shapes=[pltpu.VMEM((PEER_H, PEER_KEYS, TB1), F32), pltpu.VMEM((PEER_H, PEER_KEYS, TB1), F32)],
        compiler_params=pltpu.CompilerParams(dimension_semantics=("arbitrary",), vmem_limit_bytes=VMEM_LIMIT),
        name="peer_route",
    )(xall, mods4, W["norm2"], W["peer_w_q"], W["sk1"], W["sk2"])


TB2 = 512
KEYS_PER_STEP = 8
EB = KEYS_PER_STEP * PEER_KEYS


def _cast_kernel(x_ref, o_ref):
    o_ref[...] = x_ref[...].astype(BF16)


def _to_bf16(tab):
    rows = 2048
    return pl.pallas_call(
        _cast_kernel,
        out_shape=jax.ShapeDtypeStruct(tab.shape, BF16),
        grid=(DEPTH, N_EXPERTS // rows),
        in_specs=[pl.BlockSpec((None, rows, D), lambda l, i: (l, i, 0))],
        out_specs=pl.BlockSpec((None, rows, D), lambda l, i: (l, i, 0)),
        name="expert_table_to_bf16",
    )(tab)


def _expert_kernel(final_norm, x_ref, mod_ref, h_ref, n_ref, e1_ref, b_ref, e2_ref, u_ref, v_ref, nf_ref,
                   o_ref, acc_s, a_s, coef_s):
    e = pl.program_id(1)

    @pl.when(e == 0)
    def _():
        acc_s[...] = jnp.zeros(acc_s.shape, F32)

    a_s[...] = _dot_nt(u_ref[...], h_ref[...])

    i1_base = pl.multiple_of(e * KEYS_PER_STEP, KEYS_PER_STEP)
    for c in range(TB2 // CH):
        cs = slice(c * CH, (c + 1) * CH)
        n8 = [n_ref[h, pl.ds(i1_base, KEYS_PER_STEP), cs] for h in range(PEER_H)]
        e18 = [e1_ref[h, pl.ds(i1_base, KEYS_PER_STEP), cs] for h in range(PEER_H)]
        for j in range(KEYS_PER_STEP):
            rs = slice(j * PEER_KEYS, (j + 1) * PEER_KEYS)
            g = jnp.zeros((PEER_KEYS, CH), F32)
            for h in range(PEER_H):
                g = g + jnp.where(b_ref[h, :, cs] < n8[h][j:j + 1, :], e2_ref[h, :, cs] * e18[h][j:j + 1, :], 0.0)
            coef_s[rs, cs] = (g * _gelu_tanh(a_s[rs, cs])).astype(BF16)
    acc_s[...] += _dot_tn(coef_s[...], v_ref[...])

    @pl.when(e == pl.num_programs(1) - 1)
    def _():
        g2 = mod_ref[:, 5 * D:6 * D]
        y = x_ref[...] + g2 * acc_s[...]
        if final_norm:
            y = _rms(y, nf_ref[...])
        o_ref[...] = y


def _peer_experts(xall, mods4, layer, hb, nsel, e1, rb, e2, u_bf, v_bf, norm_f, final_norm):
    nt = T_ALL // TB2

    def mod_row(i):
        start = T_PROMPT // TB2
        per = L_SAMPLE // TB2
        return jnp.where(i < start, 0, 1 + (i - start) // per)

    kt_spec = pl.BlockSpec((PEER_H, PEER_KEYS, TB2), lambda i, e: (0, 0, i))
    return pl.pallas_call(
        functools.partial(_expert_kernel, final_norm),
        out_shape=jax.ShapeDtypeStruct((T_ALL, D), F32),
        grid=(nt, N_EXPERTS // EB),
        in_specs=[
            pl.BlockSpec((TB2, D), lambda i, e: (i, 0)),
            pl.BlockSpec((None, None, 1, 6 * D), lambda i, e: (layer, mod_row(i), 0, 0)),
            pl.BlockSpec((TB2, D), lambda i, e: (i, 0)),
            kt_spec, kt_spec, kt_spec, kt_spec,
            pl.BlockSpec((None, EB, D), lambda i, e: (layer, e, 0)),
            pl.BlockSpec((None, EB, D), lambda i, e: (layer, e, 0)),
            pl.BlockSpec((1, D), lambda i, e: (0, 0)),
        ],
        out_specs=pl.BlockSpec((TB2, D), lambda i, e: (i, 0)),
        scratch_shapes=[pltpu.VMEM((TB2, D), F32), pltpu.VMEM((EB, TB2), F32), pltpu.VMEM((EB, TB2), BF16)],
        input_output_aliases={0: 0},
        compiler_params=pltpu.CompilerParams(dimension_semantics=("arbitrary", "arbitrary"),
                                             vmem_limit_bytes=VMEM_LIMIT),
        name="peer_experts",
    )(xall, mods4, hb, nsel, e1, rb, e2, u_bf, v_bf, norm_f)


def _rot_cols(w):
    c = w.reshape(w.shape[:-1] + (2, 2, ROPE // 4))
    return jnp.stack([-c[..., 1, :], c[..., 0, :]], axis=-2).reshape(w.shape)


def _layer_weights(l, norm1, norm2, w_in, conv_w, mla_q_norm, mla_w_uq, mla_kv_norm, mla_w_uk, mla_w_uv,
                   ret_decay_logit, pool_w, pool_scale, w_out, peer_w_q, peer_subkeys):
    wi = w_in[l]
    kr_cols = wi[:, 1152:1184]
    zpad = jnp.zeros((D, LANES - ROPE), F32)
    w_in_r = jnp.concatenate([wi[:, 0:1152], wi[:, 1184:2464], kr_cols, zpad, _rot_cols(kr_cols), zpad], axis=1)

    uq = mla_w_uq[l]
    z32 = jnp.zeros((Q_LORA, N_HEADS, LANES - HD - ROPE), F32)
    w_uq = jnp.concatenate([uq[..., HD:], uq[..., :HD], z32], axis=-1).reshape(Q_LORA, N_HEADS * LANES)
    w_uq_rot = jnp.concatenate([_rot_cols(uq[..., HD:]), jnp.zeros((Q_LORA, N_HEADS, LANES - ROPE), F32)],
                               axis=-1).reshape(Q_LORA, N_HEADS * LANES)
    uk = mla_w_uk[l]
    w_uk = jnp.concatenate([jnp.zeros((KV_LORA, N_HEADS, ROPE), F32), uk,
                            jnp.zeros((KV_LORA, N_HEADS, LANES - HD - ROPE), F32)], axis=-1).reshape(KV_LORA, -1)
    uv = mla_w_uv[l]
    w_uv = jnp.concatenate([uv, jnp.zeros((KV_LORA, N_HEADS, LANES - HD), F32)], axis=-1).reshape(KV_LORA, -1)

    wo = w_out[l]
    wo_m = jnp.concatenate([wo[GW:2 * GW].reshape(N_HEADS, HD, D), jnp.zeros((N_HEADS, LANES - HD, D), F32)],
                           axis=1).reshape(N_HEADS * LANES, D)
    w_out_r = jnp.concatenate([wo[0:GW], wo_m, wo[2 * GW:]], axis=0)

    pw = pool_w[l]
    pool_bd = jnp.zeros((GW, GW), F32)
    for g in range(4):
        pool_bd = pool_bd.at[g * HD:(g + 1) * HD, g * HD:(g + 1) * HD].set(pw[g])

    sk = peer_subkeys[l]
    zk = jnp.zeros((PEER_H, PEER_KEYS, HD), F32)
    return dict(
        norm1=norm1[l].reshape(1, D), norm2=norm2[l].reshape(1, D),
        w_in=w_in_r.astype(BF16), conv_w=conv_w[l],
        q_norm=mla_q_norm[l].reshape(1, Q_LORA), w_uq=w_uq.astype(BF16), w_uq_rot=w_uq_rot.astype(BF16),
        kv_norm=mla_kv_norm[l].reshape(1, KV_LORA), w_uk=w_uk.astype(BF16), w_uv=w_uv.astype(BF16),
        lg=jnp.broadcast_to(ret_decay_logit[l].reshape(8, 1), (8, L_SAMPLE)),
        pool_w=pool_bd.astype(BF16), pool_scale=pool_scale[l].reshape(1, GW), w_out=w_out_r.astype(BF16),
        peer_w_q=peer_w_q[l].astype(BF16),
        sk1=jnp.concatenate([sk[:, 0], zk], axis=-1).astype(BF16),
        sk2=jnp.concatenate([zk, sk[:, 1]], axis=-1).astype(BF16),
    )


def _rope_slot_tables(L):
    rows = L // GRID_W
    row = jnp.repeat(jnp.arange(rows, dtype=F32), GRID_W)
    col = jnp.tile(jnp.arange(GRID_W, dtype=F32), rows)
    pairs = ROPE // 4
    inv = ROPE_BASE ** (-jnp.arange(pairs, dtype=F32) / pairs)
    ang_r = row[:, None] * inv
    ang_c = col[:, None] * inv
    ang = jnp.concatenate([ang_r, ang_r, ang_c, ang_c], axis=-1)
    cos = jnp.concatenate([jnp.cos(ang), jnp.ones((L, LANES - ROPE), F32)], axis=1)
    sin = jnp.concatenate([jnp.sin(ang), jnp.zeros((L, LANES - ROPE), F32)], axis=1)
    return cos, sin


def _slot_states(st):
    b = st.shape[0]
    s = st.reshape(b, 2, N_HEADS // 2, 2, HD, HD)
    z = jnp.zeros_like(s[:, :, :, 0])
    top = jnp.concatenate([s[:, :, :, 0], z], axis=-1)
    bot = jnp.concatenate([z, s[:, :, :, 1]], axis=-1)
    return jnp.concatenate([top, bot], axis=-2)


def kernel(x_prompt, x_sample, cache_mla_ckv, cache_mla_krope, state_ret, c, c_ctx, w_mod, b_mod, norm1, norm2,
           w_in, conv_w, mla_q_norm, mla_w_uq, mla_kv_norm, mla_w_uk, mla_w_uv, ret_decay_logit, pool_w,
           pool_scale, w_out, peer_w_q, peer_subkeys, peer_u, peer_v, norm_f):
    c8 = jnp.concatenate([c_ctx.reshape(1, D), c, jnp.zeros((5, D), F32)], axis=0)
    mods4 = _modulation(c8, w_mod, b_mod).reshape(DEPTH, 8, 1, 6 * D)
    u_bf = _to_bf16(peer_u)
    v_bf = _to_bf16(peer_v)
    cos, sin = _rope_slot_tables(L_SAMPLE)
    krc_pad = jnp.pad(cache_mla_krope, ((0, 0), (0, 0), (0, 0), (0, LANES - ROPE)))

    xall = jnp.concatenate([x_prompt.reshape(T_PROMPT, D), x_sample.reshape(T_SAMPLE, D)], axis=0)
    ckv_l, kr_l, st_l = [], [], []
    for l in range(DEPTH):
        W = _layer_weights(l, norm1, norm2, w_in, conv_w, mla_q_norm, mla_w_uq, mla_kv_norm, mla_w_uk, mla_w_uv,
                           ret_decay_logit, pool_w, pool_scale, w_out, peer_w_q, peer_subkeys)
        xall, ckv, kr, st = _token_mix(xall, mods4, l, W, L_PROMPT, N_PROMPT_SEQ, 0, lambda b: 0, None)
        ctx = (cos, sin, cache_mla_ckv[:, l], krc_pad[:, l], _slot_states(state_ret[:, l]))
        (xall,) = _token_mix(xall, mods4, l, W, L_SAMPLE, N_SAMPLE_SEQ, T_PROMPT // L_SAMPLE,
                             lambda b: 1 + b, ctx)
        hb, nsel, e1, rb, e2 = _peer_route(xall, mods4, l, W)
        xall = _peer_experts(xall, mods4, l, hb, nsel, e1, rb, e2, u_bf, v_bf, norm_f.reshape(1, D),
                             final_norm=(l == DEPTH - 1))
        ckv_l.append(ckv)
        kr_l.append(kr)
        s6 = st.reshape(N_PROMPT_SEQ, 2, N_HEADS // 2, 2, HD, 2, HD)
        st_l.append(jnp.stack([s6[:, :, :, 0, :, 0, :], s6[:, :, :, 1, :, 1, :]], axis=3)
                    .reshape(N_PROMPT_SEQ, 2, N_HEADS, HD, HD))
    y_prompt = xall[:T_PROMPT].reshape(N_PROMPT_SEQ, L_PROMPT, D)
    y_sample = xall[T_PROMPT:].reshape(N_SAMPLE_SEQ, L_SAMPLE, D)
    return (y_prompt, y_sample, jnp.stack(ckv_l, axis=1), jnp.stack(kr_l, axis=1), jnp.stack(st_l, axis=1))
```
